```python
import math
import jax, jax.numpy as jnp
from jax import lax
import numpy as np

D_MODEL = 1024
BATCH = 4
SEQ = 8192
DEPTH = 2

N_MIXERS = 2
N_Q_HEADS = 8
N_KV_HEADS = 2
HEAD_DIM = D_MODEL // N_Q_HEADS
Q_PER_KV = N_Q_HEADS // N_KV_HEADS
ROT_DIM = HEAD_DIM // 4
ROPE_THETA = 500000.0
WINDOW = 128
BLOCK = 128
D_RNN = 3 * D_MODEL // 2
N_RNN_BLOCKS = 16
RNN_BLOCK_W = D_RNN // N_RNN_BLOCKS
CONV_W = 4
CONV_LEFT = 2
LRU_C = 8.0
D_FF = 4 * D_MODEL
DN_ALPHA = (2.0 * DEPTH) ** 0.25
DN_BETA = (8.0 * DEPTH) ** -0.25
LN_EPS = 1e-5
ADA_INIT = 0.5

N_ATTN_LAYERS = (DEPTH + N_MIXERS - 1) // N_MIXERS
N_RNN_LAYERS = DEPTH // N_MIXERS

kernel_name = "hybrid_swa_rglru_adaln_deepnorm_encoder"


def layer_norm(x, g, b):
    xf = x.astype(jnp.float32)
    mu = jnp.mean(xf, axis=-1, keepdims=True)
    var = jnp.mean(jnp.square(xf - mu), axis=-1, keepdims=True)
    y = (xf - mu) * lax.rsqrt(var + LN_EPS)
    return (y * g.astype(jnp.float32) + b.astype(jnp.float32)).astype(x.dtype)


def ada_modulation(c, w, b):
    mod = jax.nn.silu(c) @ w + b
    shift, scale, gate = jnp.split(mod, 3, axis=-1)
    return shift[:, None, :], scale[:, None, :], gate[:, None, :]


def partial_rope(t, cos, sin):
    half = ROT_DIM // 2
    tr = t[..., :ROT_DIM].astype(jnp.float32)
    t1, t2 = tr[..., :half], tr[..., half:]
    cs, sn = cos[None, :, None, :], sin[None, :, None, :]
    rot = jnp.concatenate([t1 * cs - t2 * sn, t2 * cs + t1 * sn], axis=-1).astype(t.dtype)
    return jnp.concatenate([rot, t[..., ROT_DIM:]], axis=-1)


def windowed_gqa(h, w_in, w_out, sinks):
    B, S, _ = h.shape
    nblk = S // BLOCK
    qkv = h @ w_in
    q, k, v = jnp.split(qkv, [N_Q_HEADS * HEAD_DIM, (N_Q_HEADS + N_KV_HEADS) * HEAD_DIM], axis=-1)
    q = q.reshape(B, S, N_Q_HEADS, HEAD_DIM)
    k = k.reshape(B, S, N_KV_HEADS, HEAD_DIM)
    v = v.reshape(B, S, N_KV_HEADS, HEAD_DIM)
    pos = jnp.arange(S, dtype=jnp.float32)
    inv_freq = ROPE_THETA ** (-jnp.arange(0, ROT_DIM, 2, dtype=jnp.float32) / ROT_DIM)
    ang = pos[:, None] * inv_freq[None, :]
    cos, sin = jnp.cos(ang), jnp.sin(ang)
    q = partial_rope(q, cos, sin)
    k = partial_rope(k, cos, sin)
    qb = q.reshape(B, nblk, BLOCK, N_KV_HEADS, Q_PER_KV, HEAD_DIM)

    def band(t):
        tp = jnp.pad(t, ((0, 0), (BLOCK, BLOCK), (0, 0), (0, 0)))
        tb = tp.reshape(B, nblk + 2, BLOCK, N_KV_HEADS, HEAD_DIM)
        return jnp.concatenate([tb[:, :-2], tb[:, 1:-1], tb[:, 2:]], axis=2)

    kb, vb = band(k), band(v)
    scores = jnp.einsum("bnqhgd,bnjhd->bnhgqj", qb, kb,
                        preferred_element_type=jnp.float32) * (HEAD_DIM ** -0.5)
    blk = jnp.arange(nblk)[:, None, None] * BLOCK
    qpos = blk + jnp.arange(BLOCK)[None, :, None]
    kpos = blk - BLOCK + jnp.arange(3 * BLOCK)[None, None, :]
    valid = (jnp.abs(qpos - kpos) <= WINDOW) & (kpos >= 0) & (kpos < S)
    scores = jnp.where(valid[None, :, None, None], scores, -jnp.inf)
    sink = sinks.astype(jnp.float32).reshape(N_KV_HEADS, Q_PER_KV)[None, None, :, :, None, None]
    m = jnp.maximum(jnp.max(scores, axis=-1, keepdims=True), sink)
    p = jnp.exp(scores - m)
    denom = jnp.sum(p, axis=-1, keepdims=True) + jnp.exp(sink - m)
    probs = (p / denom).astype(v.dtype)
    o = jnp.einsum("bnhgqj,bnjhd->bnqhgd", probs, vb)
    o = o.reshape(B, S, N_Q_HEADS * HEAD_DIM)
    return o @ w_out


def centred_depthwise_conv(x, w, b):
    S = x.shape[1]
    xp = jnp.pad(x, ((0, 0), (CONV_LEFT, CONV_W - 1 - CONV_LEFT), (0, 0)))
    out = xp[:, 0:S] * w[0] + b
    for j in range(1, CONV_W):
        out = out + xp[:, j:j + S] * w[j]
    return out


def rg_lru(x, w_a, b_a, w_x, b_x, lam, reverse):
    B, S, _ = x.shape
    xb = x.reshape(B, S, N_RNN_BLOCKS, RNN_BLOCK_W)
    r = jax.nn.sigmoid((jnp.einsum("bsni,nij->bsnj", xb, w_a).reshape(B, S, D_RNN) + b_a).astype(jnp.float32))
    i = jax.nn.sigmoid((jnp.einsum("bsni,nij->bsnj", xb, w_x).reshape(B, S, D_RNN) + b_x).astype(jnp.float32))
    log_a = -LRU_C * r * jax.nn.softplus(-lam.astype(jnp.float32))
    a = jnp.exp(log_a)
    mult = jnp.sqrt(-jnp.expm1(2.0 * log_a))
    u = mult * (i * x.astype(jnp.float32))

    def combine(left, right):
        a_l, b_l = left
        a_r, b_r = right
        return a_l * a_r, a_r * b_l + b_r

    _, hs = lax.associative_scan(combine, (a, u), axis=1, reverse=reverse)
    return hs


def recurrent_block(h, w_in, conv_w, conv_b, w_a, b_a, w_x, b_x, lam, w_out):
    z = h @ w_in
    xr, gate = jnp.split(z, 2, axis=-1)
    xr = centred_depthwise_conv(xr, conv_w, conv_b)
    y = (rg_lru(xr, w_a[0], b_a[0], w_x[0], b_x[0], lam[0], reverse=False)
         + rg_lru(xr, w_a[1], b_a[1], w_x[1], b_x[1], lam[1], reverse=True))
    y = y.astype(h.dtype) * jax.nn.gelu(gate)
    return y @ w_out


def sq_relu_mlp(h, w1, w2):
    return jnp.square(jax.nn.relu(h @ w1)) @ w2


def setup_inputs(seed: int = 0) -> dict:
    key = jax.random.key(seed)
    ks = jax.random.split(key, 24)
    nrm = lambda k, shape, s: jax.random.normal(k, shape, jnp.float32) * s
    u = jax.random.uniform(ks[17], (N_RNN_LAYERS, 2, D_RNN), jnp.float32, 0.9, 0.999)
    s_lam = u ** (1.0 / LRU_C)
    lam = jnp.log(s_lam) - jnp.log1p(-s_lam)
    return {
        "x": nrm(ks[0], (BATCH, SEQ, D_MODEL), 1.0),
        "c": nrm(ks[1], (BATCH, D_MODEL), 1.0),
        "ada_w": nrm(ks[2], (DEPTH, 2, D_MODEL, 3 * D_MODEL), ADA_INIT * D_MODEL ** -0.5),
        "ada_b": nrm(ks[3], (DEPTH, 2, 3 * D_MODEL), 0.01),
        "ln_g": 1.0 + nrm(ks[4], (DEPTH, 2, D_MODEL), 0.02),
        "ln_b": nrm(ks[5], (DEPTH, 2, D_MODEL), 0.02),
        "attn_w_in": nrm(ks[6], (N_ATTN_LAYERS, D_MODEL, (N_Q_HEADS + 2 * N_KV_HEADS) * HEAD_DIM), D_MODEL ** -0.5),
        "attn_w_out": nrm(ks[7], (N_ATTN_LAYERS, N_Q_HEADS * HEAD_DIM, D_MODEL), DN_BETA * (N_Q_HEADS * HEAD_DIM) ** -0.5),
        "attn_sinks": nrm(ks[8], (N_ATTN_LAYERS, N_Q_HEADS), 1.0),
        "rnn_w_in": nrm(ks[9], (N_RNN_LAYERS, D_MODEL, 2 * D_RNN), D_MODEL ** -0.5),
        "rnn_conv_w": nrm(ks[10], (N_RNN_LAYERS, CONV_W, D_RNN), CONV_W ** -0.5),
        "rnn_conv_b": nrm(ks[11], (N_RNN_LAYERS, D_RNN), 0.01),
        "rnn_w_a": nrm(ks[12], (N_RNN_LAYERS, 2, N_RNN_BLOCKS, RNN_BLOCK_W, RNN_BLOCK_W), RNN_BLOCK_W ** -0.5),
        "rnn_b_a": nrm(ks[13], (N_RNN_LAYERS, 2, D_RNN), 0.01),
        "rnn_w_x": nrm(ks[14], (N_RNN_LAYERS, 2, N_RNN_BLOCKS, RNN_BLOCK_W, RNN_BLOCK_W), RNN_BLOCK_W ** -0.5),
        "rnn_b_x": nrm(ks[15], (N_RNN_LAYERS, 2, D_RNN), 0.01),
        "rnn_lam": lam,
        "rnn_w_out": nrm(ks[16], (N_RNN_LAYERS, D_RNN, D_MODEL), DN_BETA * D_RNN ** -0.5),
        "mlp_w1": nrm(ks[18], (DEPTH, D_MODEL, D_FF), D_MODEL ** -0.5),
        "mlp_w2": nrm(ks[19], (DEPTH, D_FF, D_MODEL), DN_BETA * D_FF ** -0.5),
    }


def reference(x, c, ada_w, ada_b, ln_g, ln_b, attn_w_in, attn_w_out, attn_sinks,
              rnn_w_in, rnn_conv_w, rnn_conv_b, rnn_w_a, rnn_b_a, rnn_w_x, rnn_b_x, rnn_lam,
              rnn_w_out, mlp_w1, mlp_w2):
    for i in range(DEPTH):
        j = i // N_MIXERS
        shift, scale, gate = ada_modulation(c, ada_w[i, 0], ada_b[i, 0])
        h = x * (1.0 + scale) + shift
        if i % N_MIXERS == 0:
            y = windowed_gqa(h, attn_w_in[j], attn_w_out[j], attn_sinks[j])
        else:
            y = recurrent_block(h, rnn_w_in[j], rnn_conv_w[j], rnn_conv_b[j], rnn_w_a[j], rnn_b_a[j],
                                rnn_w_x[j], rnn_b_x[j], rnn_lam[j], rnn_w_out[j])
        x = layer_norm(DN_ALPHA * x + (1.0 + gate) * y, ln_g[i, 0], ln_b[i, 0])
        shift, scale, gate = ada_modulation(c, ada_w[i, 1], ada_b[i, 1])
        y = sq_relu_mlp(x * (1.0 + scale) + shift, mlp_w1[i], mlp_w2[i])
        x = layer_norm(DN_ALPHA * x + (1.0 + gate) * y, ln_g[i, 1], ln_b[i, 1])
    return x
```

```python
import functools
import math

import jax
import jax.numpy as jnp
from jax import lax
from jax.experimental import pallas as pl
from jax.experimental.pallas import tpu as pltpu

D_MODEL = 1024
DEPTH = 2
N_Q_HEADS = 8
N_KV_HEADS = 2
HEAD_DIM = 128
Q_PER_KV = N_Q_HEADS // N_KV_HEADS
ROT_DIM = 32
ROPE_THETA = 500000.0
WINDOW = 128
D_RNN = 1536
N_RNN_BLOCKS = 16
RNN_BLOCK_W = D_RNN // N_RNN_BLOCKS
CONV_W = 4
LRU_C = 8.0
D_FF = 4 * D_MODEL
DN_ALPHA = (2.0 * DEPTH) ** 0.25
LN_EPS = 1e-5

SUBLANES = 8
LANES = 128
TM = 512
SEG_LEN = TM // SUBLANES
GATE_GROUP = 384
N_GATE_GROUPS = D_RNN // GATE_GROUP
HALO_ROWS = 32
FF_CHUNK = 1024
VMEM_LIMIT = 60 * 1024 * 1024

_BF16 = jnp.bfloat16
_F32 = jnp.float32


def _const_spec(shape):
    return pl.BlockSpec(shape, lambda *_: (0,) * len(shape))


def _params(n_axes):
    return pltpu.CompilerParams(dimension_semantics=("arbitrary",) * n_axes,
                                vmem_limit_bytes=VMEM_LIMIT)


def _layer_norm(r, g, b):
    mu = jnp.mean(r, axis=-1, keepdims=True)
    d = r - mu
    var = jnp.mean(d * d, axis=-1, keepdims=True)
    return d * lax.rsqrt(var + LN_EPS) * g + b


def _sigmoid(x):
    return 0.5 * jnp.tanh(0.5 * x) + 0.5


def _gelu_tanh(x):
    c = math.sqrt(2.0 / math.pi)
    return x * (0.5 * (1.0 + jnp.tanh(c * (x + 0.044715 * (x * x * x)))))


def _ada_kernel(c_ref, w_ref, b_ref, o_ref):
    c = c_ref[...]
    sc = (c * _sigmoid(c)).astype(_BF16)
    o_ref[0] = jnp.dot(sc, w_ref[0].astype(_BF16), preferred_element_type=_F32) + b_ref[0]


def _ada_modulation(c, ada_w, ada_b):
    batch = c.shape[0]
    n_sub = DEPTH * 2
    rows = -(-batch // SUBLANES) * SUBLANES
    c_pad = jnp.pad(c, ((0, rows - batch), (0, 0)))
    w = ada_w.reshape(n_sub, D_MODEL, 3 * D_MODEL)
    b = ada_b.reshape(n_sub, 1, 3 * D_MODEL)
    out = pl.pallas_call(
        _ada_kernel,
        grid=(n_sub, 3),
        in_specs=[
            _const_spec((rows, D_MODEL)),
            pl.BlockSpec((1, D_MODEL, D_MODEL), lambda i, j: (i, 0, j)),
            pl.BlockSpec((1, 1, D_MODEL), lambda i, j: (i, 0, j)),
        ],
        out_specs=pl.BlockSpec((1, rows, D_MODEL), lambda i, j: (i, 0, j)),
        out_shape=jax.ShapeDtypeStruct((n_sub, rows, 3 * D_MODEL), _F32),
        compiler_params=_params(2),
        name="ada_modulation",
    )(c_pad, w, b)
    return out[:, :batch, None, :]


def _qkv_kernel(x_ref, mod_ref, w_ref, rc_ref, ra_ref, rb_ref, q_ref, k_ref, v_ref):
    mod = mod_ref[0]
    shift, scale = mod[:, :D_MODEL], mod[:, D_MODEL:2 * D_MODEL]
    h = (x_ref[0] * (1.0 + scale) + shift).astype(_BF16)
    z = jnp.dot(h, w_ref[...], preferred_element_type=_F32)
    rc, ra, rb = rc_ref[...], ra_ref[...], rb_ref[...]
    q_scale = HEAD_DIM ** -0.5
    for head in range(N_Q_HEADS + N_KV_HEADS):
        t = z[:, head * HEAD_DIM:(head + 1) * HEAD_DIM]
        rot = t * rc + pltpu.roll(t, LANES - ROT_DIM // 2, 1) * ra + pltpu.roll(t, ROT_DIM // 2, 1) * rb
        if head < N_Q_HEADS:
            q_ref[0, :, head * HEAD_DIM:(head + 1) * HEAD_DIM] = (rot * q_scale).astype(_BF16)
        else:
            kh = head - N_Q_HEADS
            k_ref[0, :, kh * HEAD_DIM:(kh + 1) * HEAD_DIM] = rot.astype(_BF16)
    v_ref[0] = z[:, (N_Q_HEADS + N_KV_HEADS) * HEAD_DIM:].astype(_BF16)


def _rope_tables(seq):
    half = ROT_DIM // 2
    pos = jnp.arange(seq, dtype=_F32)
    inv_freq = ROPE_THETA ** (-jnp.arange(0, ROT_DIM, 2, dtype=_F32) / ROT_DIM)
    ang = pos[:, None] * inv_freq[None, :]
    cos, sin = jnp.cos(ang), jnp.sin(ang)
    pad = LANES - ROT_DIM
    zeros = jnp.zeros((seq, half), _F32)
    rc = jnp.concatenate([cos, cos, jnp.ones((seq, pad), _F32)], axis=1)
    ra = jnp.concatenate([-sin, zeros, jnp.zeros((seq, pad), _F32)], axis=1)
    rb = jnp.concatenate([zeros, sin, jnp.zeros((seq, pad), _F32)], axis=1)
    return rc, ra, rb


def _qkv_projection(x, mod, w_in):
    batch, seq, _ = x.shape
    n_q, n_kv = N_Q_HEADS * HEAD_DIM, N_KV_HEADS * HEAD_DIM
    rc, ra, rb = _rope_tables(seq)
    tok = lambda width: pl.BlockSpec((1, TM, width), lambda b, i: (b, i, 0))
    rope = pl.BlockSpec((TM, LANES), lambda b, i: (i, 0))
    return pl.pallas_call(
        _qkv_kernel,
        grid=(batch, seq // TM),
        in_specs=[tok(D_MODEL), pl.BlockSpec((1, 1, 3 * D_MODEL), lambda b, i: (b, 0, 0)),
                  _const_spec(w_in.shape), rope, rope, rope],
        out_specs=[tok(n_q), tok(n_kv), tok(n_kv)],
        out_shape=[jax.ShapeDtypeStruct((batch, seq, n_q), _BF16),
                   jax.ShapeDtypeStruct((batch, seq, n_kv), _BF16),
                   jax.ShapeDtypeStruct((batch, seq, n_kv), _BF16)],
        compiler_params=_params(2),
        name="attn_qkv_rope",
    )(x, mod, w_in, rc, ra, rb)


def _attn_kernel(sink_ref, q_ref, kp_ref, kc_ref, kn_ref, vp_ref, vc_ref, vn_ref, x_ref, mod_ref,
                 wo_ref, g_ref, b_ref, o_ref, kband, vband, attn_o, *, n_blocks):
    i = pl.program_id(1)
    kband[0:WINDOW] = kp_ref[0]
    kband[WINDOW:WINDOW + TM] = kc_ref[0]
    kband[WINDOW + TM:] = kn_ref[0]
    vband[0:WINDOW] = vp_ref[0]
    vband[WINDOW:WINDOW + TM] = vc_ref[0]
    vband[WINDOW + TM:] = vn_ref[0]

    rows = Q_PER_KV * WINDOW
    band = 3 * WINDOW
    r = lax.broadcasted_iota(jnp.int32, (rows, band), 0) & (WINDOW - 1)
    c = lax.broadcasted_iota(jnp.int32, (rows, band), 1)
    in_window = (c >= r) & (c <= r + 2 * WINDOW)
    row_head = lax.broadcasted_iota(jnp.int32, (rows, 1), 0) // WINDOW
    for j in range(TM // WINDOW):
        blk = i * (TM // WINDOW) + j
        lo = jnp.where(blk == 0, WINDOW, 0)
        hi = jnp.where(blk == n_blocks - 1, 2 * WINDOW, band)
        valid = in_window & (c >= lo) & (c < hi)
        for kvh in range(N_KV_HEADS):
            heads = [kvh * Q_PER_KV + g for g in range(Q_PER_KV)]
            qs = jnp.concatenate(
                [q_ref[0, j * WINDOW:(j + 1) * WINDOW, h * HEAD_DIM:(h + 1) * HEAD_DIM] for h in heads], axis=0)
            kb = kband[j * WINDOW:j * WINDOW + band, kvh * HEAD_DIM:(kvh + 1) * HEAD_DIM]
            vb = vband[j * WINDOW:j * WINDOW + band, kvh * HEAD_DIM:(kvh + 1) * HEAD_DIM]
            s = lax.dot_general(qs, kb, (((1,), (1,)), ((), ())), preferred_element_type=_F32)
            s = jnp.where(valid, s, -jnp.inf)
            sink = jnp.full((rows, 1), sink_ref[heads[0]], _F32)
            for g in range(1, Q_PER_KV):
                sink = jnp.where(row_head == g, sink_ref[heads[g]], sink)
            m = jnp.maximum(jnp.max(s, axis=-1, keepdims=True), sink)
            p = jnp.exp(s - m)
            denom = jnp.sum(p, axis=-1, keepdims=True) + jnp.exp(sink - m)
            o = jnp.dot(p.astype(_BF16), vb, preferred_element_type=_F32) / denom
            for g, h in enumerate(heads):
                attn_o[j * WINDOW:(j + 1) * WINDOW, h * HEAD_DIM:(h + 1) * HEAD_DIM] = (
                    o[g * WINDOW:(g + 1) * WINDOW].astype(_BF16))

    y = jnp.dot(attn_o[...], wo_ref[...], preferred_element_type=_F32)
    gate = mod_ref[0][:, 2 * D_MODEL:]
    o_ref[0] = _layer_norm(DN_ALPHA * x_ref[0] + (1.0 + gate) * y, g_ref[...], b_ref[...])


def _attention_block(x, q, k, v, mod, w_out, sinks, ln_g, ln_b):
    batch, seq, _ = x.shape
    n_kv = N_KV_HEADS * HEAD_DIM
    per_tile = TM // WINDOW
    n_blocks = seq // WINDOW
    tok = lambda width: pl.BlockSpec((1, TM, width), lambda b, i: (b, i, 0))
    prev = pl.BlockSpec((1, WINDOW, n_kv), lambda b, i: (b, jnp.maximum(i * per_tile - 1, 0), 0))
    nxt = pl.BlockSpec((1, WINDOW, n_kv), lambda b, i: (b, jnp.minimum((i + 1) * per_tile, n_blocks - 1), 0))
    return pl.pallas_call(
        functools.partial(_attn_kernel, n_blocks=n_blocks),
        grid=(batch, seq // TM),
        in_specs=[pl.BlockSpec(memory_space=pltpu.SMEM),
                  tok(N_Q_HEADS * HEAD_DIM), prev, tok(n_kv), nxt, prev, tok(n_kv), nxt,
                  tok(D_MODEL), pl.BlockSpec((1, 1, 3 * D_MODEL), lambda b, i: (b, 0, 0)),
                  _const_spec(w_out.shape), _const_spec((1, D_MODEL)), _const_spec((1, D_MODEL))],
        out_specs=tok(D_MODEL),
        out_shape=jax.ShapeDtypeStruct(x.shape, _F32),
        scratch_shapes=[pltpu.VMEM((TM + 2 * WINDOW, n_kv), _BF16),
                        pltpu.VMEM((TM + 2 * WINDOW, n_kv), _BF16),
                        pltpu.VMEM((TM, N_Q_HEADS * HEAD_DIM), _BF16)],
        compiler_params=_params(2),
        name="attn_band_out_ln",
    )(sinks, q, k, k, k, v, v, v, x, mod, w_out, ln_g, ln_b)


def _mlp_kernel(x_ref, mod_ref, w1_ref, w2_ref, g_ref, b_ref, o_ref, acc_ref):
    x = x_ref[0]
    mod = mod_ref[0]
    shift, scale, gate = mod[:, :D_MODEL], mod[:, D_MODEL:2 * D_MODEL], mod[:, 2 * D_MODEL:]
    h = (x * (1.0 + scale) + shift).astype(_BF16)
    for c in range(D_FF // FF_CHUNK):
        cols = slice(c * FF_CHUNK, (c + 1) * FF_CHUNK)
        hid = jnp.maximum(jnp.dot(h, w1_ref[:, cols], preferred_element_type=_F32), 0.0)
        part = jnp.dot((hid * hid).astype(_BF16), w2_ref[cols, :], preferred_element_type=_F32)
        if c == 0:
            acc_ref[...] = part
        else:
            acc_ref[...] += part
    o_ref[0] = _layer_norm(DN_ALPHA * x + (1.0 + gate) * acc_ref[...], g_ref[...], b_ref[...])


def _mlp_block(x, mod, w1, w2, ln_g, ln_b):
    batch, seq, _ = x.shape
    tok = pl.BlockSpec((1, TM, D_MODEL), lambda b, i: (b, i, 0))
    return pl.pallas_call(
        _mlp_kernel,
        grid=(batch, seq // TM),
        in_specs=[tok, pl.BlockSpec((1, 1, 3 * D_MODEL), lambda b, i: (b, 0, 0)),
                  _const_spec(w1.shape), _const_spec(w2.shape),
                  _const_spec((1, D_MODEL)), _const_spec((1, D_MODEL))],
        out_specs=tok,
        out_shape=jax.ShapeDtypeStruct(x.shape, _F32),
        scratch_shapes=[pltpu.VMEM((TM, D_MODEL), _F32)],
        compiler_params=_params(2),
        name="mlp_relu2_ln",
    )(x, mod, w1, w2, ln_g, ln_b)


def _gate_inputs(xc, wg_ref, ba_ref, bx_ref, lam_ref, a_ref, u_ref):
    xcb = xc.astype(_BF16)
    neg_lam = -lam_ref[...]
    softplus = jnp.maximum(neg_lam, 0.0) + jnp.log1p(jnp.exp(-jnp.abs(neg_lam)))
    for j in range(N_GATE_GROUPS):
        cols = slice(j * GATE_GROUP, (j + 1) * GATE_GROUP)
        pre = jnp.dot(xcb[:, cols], wg_ref[j], preferred_element_type=_F32)
        r = _sigmoid(pre[:, :GATE_GROUP] + ba_ref[:, cols])
        gi = _sigmoid(pre[:, GATE_GROUP:] + bx_ref[:, cols])
        a = jnp.exp(-LRU_C * r * softplus[:, cols])
        u = jnp.sqrt(1.0 - a * a) * (gi * xc[:, cols])
        a_ref[:, cols] = a
        u_ref[:, cols] = u


def _segment_scan(a_ref, u_ref, carry_ref, first_tile, reverse):
    width = a_ref.shape[1]
    sub = lax.broadcasted_iota(jnp.int32, (SUBLANES, width), 0)

    def local(step, state):
        t = (SEG_LEN - 1 - step) if reverse else step
        rows = pl.ds(pl.multiple_of(t * SUBLANES, SUBLANES), SUBLANES)
        h, p = state
        a_t = a_ref[rows, :]
        h = a_t * h + u_ref[rows, :]
        p = p * a_t
        a_ref[rows, :] = p
        u_ref[rows, :] = h
        return h, p

    h_end, p_end = lax.fori_loop(
        0, SEG_LEN, local, (jnp.zeros((SUBLANES, width), _F32), jnp.ones((SUBLANES, width), _F32)), unroll=4)

    for d in (1, 2, 4):
        if reverse:
            take = sub < SUBLANES - d
            shift = SUBLANES - d
        else:
            take = sub >= d
            shift = d
        h_nb = jnp.where(take, pltpu.roll(h_end, shift, 0), 0.0)
        p_nb = jnp.where(take, pltpu.roll(p_end, shift, 0), 1.0)
        h_end = p_end * h_nb + h_end
        p_end = p_end * p_nb

    edge = 0 if reverse else SUBLANES - 1
    entry = SUBLANES - 1 if reverse else 0

    @pl.when(first_tile)
    def _():
        carry_ref[...] = jnp.zeros_like(carry_ref)

    c_in = carry_ref[edge:edge + 1, :]
    h_true = h_end + p_end * c_in
    carry_ref[...] = h_true
    h_in = jnp.where(sub == entry, c_in, pltpu.roll(h_true, SUBLANES - 1 if reverse else 1, 0))

    def fix(t, _):
        rows = pl.ds(pl.multiple_of(t * SUBLANES, SUBLANES), SUBLANES)
        u_ref[rows, :] = u_ref[rows, :] + a_ref[rows, :] * h_in
        return 0

    lax.fori_loop(0, SEG_LEN, fix, 0, unroll=4)


def _rnn_fwd_kernel(x_ref, xp_ref, xn_ref, mod_ref, w_ref, cw_ref, cb_ref, wg_ref, ba_ref, bx_ref, lam_ref,
                    xc_ref, gate_ref, hf_ref, lhs, a_s, u_s, carry, *, n_tiles):
    i = pl.program_id(1)
    mod = mod_ref[0]
    shift, scale = mod[:, :D_MODEL], mod[:, D_MODEL:2 * D_MODEL]
    lhs[0:TM] = (x_ref[0] * (1.0 + scale) + shift).astype(_BF16)
    halo = jnp.concatenate([xp_ref[0], xn_ref[0], jnp.zeros((SUBLANES, D_MODEL), _F32)], axis=0)
    lhs[TM:] = (halo * (1.0 + scale) + shift).astype(_BF16)

    xr_all = jnp.dot(lhs[...], w_ref[:, :D_RNN], preferred_element_type=_F32)
    gate_ref[0] = _gelu_tanh(jnp.dot(lhs[0:TM], w_ref[:, D_RNN:], preferred_element_type=_F32)).astype(_BF16)

    xr = xr_all[:TM]
    has_prev = jnp.where(i > 0, 1.0, 0.0)
    has_next = jnp.where(i < n_tiles - 1, 1.0, 0.0)
    prev2 = xr_all[TM + 7:TM + 8] * has_prev
    prev1 = xr_all[TM + 15:TM + 16] * has_prev
    next1 = xr_all[TM + 16:TM + 17] * has_next
    sub = lax.broadcasted_iota(jnp.int32, (SUBLANES, D_RNN), 0)
    row_m2 = jnp.where(sub == 0, prev2, pltpu.roll(xr[TM - 2 * SUBLANES:TM - SUBLANES], 1, 0))
    row_m1 = jnp.where(sub == 0, prev1, pltpu.roll(xr[TM - SUBLANES:], 1, 0))
    row_end = jnp.where(sub == SUBLANES - 1, next1, pltpu.roll(xr[:SUBLANES], SUBLANES - 1, 0))
    ext = jnp.concatenate([row_m2, row_m1, xr, row_end], axis=0)
    xc = cb_ref[...] + cw_ref[0:1, :] * ext[0:TM]
    for tap in range(1, CONV_W):
        xc = xc + cw_ref[tap:tap + 1, :] * ext[tap * SUBLANES:tap * SUBLANES + TM]
    xc_ref[0] = xc

    _gate_inputs(xc, wg_ref, ba_ref, bx_ref, lam_ref, a_s, u_s)
    _segment_scan(a_s, u_s, carry, i == 0, reverse=False)
    hf_ref[0] = u_s[...]


def _rnn_forward(xp, mod, w_in, conv_w, conv_b, wg, b_a, b_x, lam):
    batch, seq, _ = xp.shape
    n_tiles = seq // TM
    tok = lambda width: pl.BlockSpec((1, TM, width), lambda b, i: (b, i, 0))
    prev = pl.BlockSpec((1, 2 * SUBLANES, D_MODEL),
                        lambda b, i: (b, jnp.maximum(i * (TM // (2 * SUBLANES)) - 1, 0), 0))
    nxt = pl.BlockSpec((1, SUBLANES, D_MODEL),
                       lambda b, i: (b, jnp.minimum((i + 1) * SEG_LEN, seq // SUBLANES - 1), 0))
    vec = _const_spec((1, D_RNN))
    return pl.pallas_call(
        functools.partial(_rnn_fwd_kernel, n_tiles=n_tiles),
        grid=(batch, n_tiles),
        in_specs=[tok(D_MODEL), prev, nxt, pl.BlockSpec((1, 1, 3 * D_MODEL), lambda b, i: (b, 0, 0)),
                  _const_spec(w_in.shape), _const_spec((CONV_W, D_RNN)), vec,
                  _const_spec(wg.shape), vec, vec, vec],
        out_specs=[tok(D_RNN), tok(D_RNN), tok(D_RNN)],
        out_shape=[jax.ShapeDtypeStruct((batch, seq, D_RNN), _F32),
                   jax.ShapeDtypeStruct((batch, seq, D_RNN), _BF16),
                   jax.ShapeDtypeStruct((batch, seq, D_RNN), _F32)],
        scratch_shapes=[pltpu.VMEM((TM + HALO_ROWS, D_MODEL), _BF16),
                        pltpu.VMEM((TM, D_RNN), _F32), pltpu.VMEM((TM, D_RNN), _F32),
                        pltpu.VMEM((SUBLANES, D_RNN), _F32)],
        compiler_params=_params(2),
        name="rnn_in_conv_fwd_scan",
    )(xp, xp, xp, mod, w_in, conv_w, conv_b, wg, b_a, b_x, lam)


def _rnn_bwd_kernel(xc_ref, gate_ref, hf_ref, x_ref, mod_ref, wg_ref, ba_ref, bx_ref, lam_ref, wo_ref,
                    g_ref, b_ref, o_ref, a_s, u_s, carry):
    i = pl.program_id(1)
    _gate_inputs(xc_ref[0], wg_ref, ba_ref, bx_ref, lam_ref, a_s, u_s)
    _segment_scan(a_s, u_s, carry, i == 0, reverse=True)
    y = ((hf_ref[0] + u_s[...]) * gate_ref[0].astype(_F32)).astype(_BF16)
    y = jnp.dot(y, wo_ref[...], preferred_element_type=_F32)
    gate = mod_ref[0][:, 2 * D_MODEL:]
    o_ref[0] = _layer_norm(DN_ALPHA * x_ref[0] + (1.0 + gate) * y, g_ref[...], b_ref[...])


def _rnn_backward(xc, gate, hf, xp, mod, wg, b_a, b_x, lam, w_out, ln_g, ln_b):
    batch, seq, _ = xp.shape
    n_tiles = seq // TM
    tok = lambda width: pl.BlockSpec((1, TM, width), lambda b, i: (b, n_tiles - 1 - i, 0))
    vec = _const_spec((1, D_RNN))
    return pl.pallas_call(
        _rnn_bwd_kernel,
        grid=(batch, n_tiles),
        in_specs=[tok(D_RNN), tok(D_RNN), tok(D_RNN), tok(D_MODEL),
                  pl.BlockSpec((1, 1, 3 * D_MODEL), lambda b, i: (b, 0, 0)),
                  _const_spec(wg.shape), vec, vec, vec, _const_spec(w_out.shape),
                  _const_spec((1, D_MODEL)), _const_spec((1, D_MODEL))],
        out_specs=tok(D_MODEL),
        out_shape=jax.ShapeDtypeStruct(xp.shape, _F32),
        scratch_shapes=[pltpu.VMEM((TM, D_RNN), _F32), pltpu.VMEM((TM, D_RNN), _F32),
                        pltpu.VMEM((SUBLANES, D_RNN), _F32)],
        compiler_params=_params(2),
        name="rnn_bwd_scan_out_ln",
    )(xc, gate, hf, xp, mod, wg, b_a, b_x, lam, w_out, ln_g, ln_b)


def _gate_weights(w_a, w_x):
    per_group = GATE_GROUP // RNN_BLOCK_W
    eye = jnp.eye(per_group, dtype=w_a.dtype)

    def dense(w):
        w = w.reshape(N_GATE_GROUPS, per_group, RNN_BLOCK_W, RNN_BLOCK_W)
        return jnp.einsum("jnik,nm->jnimk", w, eye).reshape(N_GATE_GROUPS, GATE_GROUP, GATE_GROUP)

    return jnp.concatenate([dense(w_a), dense(w_x)], axis=-1).astype(_BF16)


def _permute_tokens(x):
    batch, seq, width = x.shape
    return x.reshape(batch, seq // TM, SUBLANES, SEG_LEN, width).swapaxes(2, 3).reshape(batch, seq, width)


def _unpermute_tokens(x):
    batch, seq, width = x.shape
    return x.reshape(batch, seq // TM, SEG_LEN, SUBLANES, width).swapaxes(2, 3).reshape(batch, seq, width)


def kernel(x, c, ada_w, ada_b, ln_g, ln_b, attn_w_in, attn_w_out, attn_sinks, rnn_w_in, rnn_conv_w, rnn_conv_b,
           rnn_w_a, rnn_b_a, rnn_w_x, rnn_b_x, rnn_lam, rnn_w_out, mlp_w1, mlp_w2):
    batch, seq, d_model = x.shape
    assert d_model == D_MODEL and seq % TM == 0
    mods = _ada_modulation(c, ada_w, ada_b)
    row = lambda v: v.reshape(1, -1)

    q, k, v = _qkv_projection(x, mods[0], attn_w_in[0].astype(_BF16))
    x = _attention_block(x, q, k, v, mods[0], attn_w_out[0].astype(_BF16), attn_sinks[0],
                         row(ln_g[0, 0]), row(ln_b[0, 0]))
    x = _mlp_block(x, mods[1], mlp_w1[0].astype(_BF16), mlp_w2[0].astype(_BF16), row(ln_g[0, 1]), row(ln_b[0, 1]))

    xp = _permute_tokens(x)
    wg_f = _gate_weights(rnn_w_a[0, 0], rnn_w_x[0, 0])
    wg_b = _gate_weights(rnn_w_a[0, 1], rnn_w_x[0, 1])
    xc, gate, hf = _rnn_forward(xp, mods[2], rnn_w_in[0].astype(_BF16), rnn_conv_w[0], row(rnn_conv_b[0]),
                                wg_f, row(rnn_b_a[0, 0]), row(rnn_b_x[0, 0]), row(rnn_lam[0, 0]))
    xp = _rnn_backward(xc, gate, hf, xp, mods[2], wg_b, row(rnn_b_a[0, 1]), row(rnn_b_x[0, 1]),
                       row(rnn_lam[0, 1]), rnn_w_out[0].astype(_BF16), row(ln_g[1, 0]), row(ln_b[1, 0]))
    xp = _mlp_block(xp, mods[3], mlp_w1[1].astype(_BF16), mlp_w2[1].astype(_BF16), row(ln_g[1, 1]), row(ln_b[1, 1]))
    return _unpermute_tokens(xp)
```

```python
import functools
import math

import jax
import jax.numpy as jnp
from jax import lax
from jax.experimental import pallas as pl
from jax.experimental.pallas import tpu as pltpu

D_MODEL = 1024
DEPTH = 2
N_Q_HEADS = 8
N_KV_HEADS = 2
HEAD_DIM = 128
Q_PER_KV = N_Q_HEADS // N_KV_HEADS
ROT_DIM = 32
ROPE_THETA = 500000.0
WINDOW = 128
D_RNN = 1536
N_RNN_BLOCKS = 16
RNN_BLOCK_W = D_RNN // N_RNN_BLOCKS
CONV_W = 4
LRU_C = 8.0
D_FF = 4 * D_MODEL
DN_ALPHA = (2.0 * DEPTH) ** 0.25
LN_EPS = 1e-5
LOG2_E = 1.0 / math.log(2.0)

SUBLANES = 8
LANES = 128
TM = 512
SEG_LEN = TM // SUBLANES
GATE_GROUP = 384
N_GATE_GROUPS = D_RNN // GATE_GROUP
HALO_ROWS = 32
FF_CHUNK = 1024
VMEM_LIMIT = 60 * 1024 * 1024

_BF16 = jnp.bfloat16
_F32 = jnp.float32


def _const_spec(shape):
    return pl.BlockSpec(shape, lambda *_: (0,) * len(shape))


def _params(n_axes):
    return pltpu.CompilerParams(dimension_semantics=("arbitrary",) * n_axes,
                                vmem_limit_bytes=VMEM_LIMIT)


def _layer_norm(r, g, b):
    mu = jnp.mean(r, axis=-1, keepdims=True)
    d = r - mu
    var = jnp.mean(d * d, axis=-1, keepdims=True)
    return d * lax.rsqrt(var + LN_EPS) * g + b


def _sigmoid(x):
    return 0.5 * jnp.tanh(0.5 * x) + 0.5


def _gelu_tanh(x):
    c = math.sqrt(2.0 / math.pi)
    half_x = 0.5 * x
    return half_x * jnp.tanh(x * ((x * x) * (c * 0.044715) + c)) + half_x


def _ada_kernel(c_ref, w_ref, b_ref, o_ref):
    c = c_ref[...]
    sc = (c * _sigmoid(c)).astype(_BF16)
    o_ref[0] = jnp.dot(sc, w_ref[0].astype(_BF16), preferred_element_type=_F32) + b_ref[0]


def _ada_modulation(c, ada_w, ada_b):
    batch = c.shape[0]
    n_sub = DEPTH * 2
    rows = -(-batch // SUBLANES) * SUBLANES
    c_pad = jnp.pad(c, ((0, rows - batch), (0, 0)))
    w = ada_w.reshape(n_sub, D_MODEL, 3 * D_MODEL)
    b = ada_b.reshape(n_sub, 1, 3 * D_MODEL)
    out = pl.pallas_call(
        _ada_kernel,
        grid=(n_sub, 3),
        in_specs=[
            _const_spec((rows, D_MODEL)),
            pl.BlockSpec((1, D_MODEL, D_MODEL), lambda i, j: (i, 0, j)),
            pl.BlockSpec((1, 1, D_MODEL), lambda i, j: (i, 0, j)),
        ],
        out_specs=pl.BlockSpec((1, rows, D_MODEL), lambda i, j: (i, 0, j)),
        out_shape=jax.ShapeDtypeStruct((n_sub, rows, 3 * D_MODEL), _F32),
        compiler_params=_params(2),
        name="ada_modulation",
    )(c_pad, w, b)
    return out[:, :batch, None, :]


def _qkv_kernel(x_ref, mod_ref, w_ref, rc_ref, ra_ref, rb_ref, q_ref, k_ref, v_ref):
    mod = mod_ref[0]
    shift, scale = mod[:, :D_MODEL], mod[:, D_MODEL:2 * D_MODEL]
    h = (x_ref[0] * (1.0 + scale) + shift).astype(_BF16)
    z = jnp.dot(h, w_ref[...], preferred_element_type=_F32)
    rc, ra, rb = rc_ref[...], ra_ref[...], rb_ref[...]
    q_scale = HEAD_DIM ** -0.5
    for head in range(N_Q_HEADS + N_KV_HEADS):
        t = z[:, head * HEAD_DIM:(head + 1) * HEAD_DIM]
        rot = t * rc + pltpu.roll(t, LANES - ROT_DIM // 2, 1) * ra + pltpu.roll(t, ROT_DIM // 2, 1) * rb
        if head < N_Q_HEADS:
            q_ref[0, :, head * HEAD_DIM:(head + 1) * HEAD_DIM] = (rot * q_scale).astype(_BF16)
        else:
            kh = head - N_Q_HEADS
            k_ref[0, :, kh * HEAD_DIM:(kh + 1) * HEAD_DIM] = rot.astype(_BF16)
    v_ref[0] = z[:, (N_Q_HEADS + N_KV_HEADS) * HEAD_DIM:].astype(_BF16)


def _rope_tables(seq):
    half = ROT_DIM // 2
    pos = jnp.arange(seq, dtype=_F32)
    inv_freq = ROPE_THETA ** (-jnp.arange(0, ROT_DIM, 2, dtype=_F32) / ROT_DIM)
    ang = pos[:, None] * inv_freq[None, :]
    cos, sin = jnp.cos(ang), jnp.sin(ang)
    pad = LANES - ROT_DIM
    zeros = jnp.zeros((seq, half), _F32)
    rc = jnp.concatenate([cos, cos, jnp.ones((seq, pad), _F32)], axis=1)
    ra = jnp.concatenate([-sin, zeros, jnp.zeros((seq, pad), _F32)], axis=1)
    rb = jnp.concatenate([zeros, sin, jnp.zeros((seq, pad), _F32)], axis=1)
    return rc, ra, rb


def _qkv_projection(x, mod, w_in):
    batch, seq, _ = x.shape
    n_q, n_kv = N_Q_HEADS * HEAD_DIM, N_KV_HEADS * HEAD_DIM
    rc, ra, rb = _rope_tables(seq)
    tok = lambda width: pl.BlockSpec((1, TM, width), lambda b, i: (b, i, 0))
    rope = pl.BlockSpec((TM, LANES), lambda b, i: (i, 0))
    return pl.pallas_call(
        _qkv_kernel,
        grid=(batch, seq // TM),
        in_specs=[tok(D_MODEL), pl.BlockSpec((1, 1, 3 * D_MODEL), lambda b, i: (b, 0, 0)),
                  _const_spec(w_in.shape), rope, rope, rope],
        out_specs=[tok(n_q), tok(n_kv), tok(n_kv)],
        out_shape=[jax.ShapeDtypeStruct((batch, seq, n_q), _BF16),
                   jax.ShapeDtypeStruct((batch, seq, n_kv), _BF16),
                   jax.ShapeDtypeStruct((batch, seq, n_kv), _BF16)],
        compiler_params=_params(2),
        name="attn_qkv_rope",
    )(x, mod, w_in, rc, ra, rb)


def _attn_kernel(sink_ref, q_ref, kp_ref, kc_ref, kn_ref, vp_ref, vc_ref, vn_ref, x_ref, mod_ref,
                 wo_ref, g_ref, b_ref, o_ref, kband, vband, attn_o, *, n_blocks):
    i = pl.program_id(1)
    kband[0:WINDOW] = kp_ref[0]
    kband[WINDOW:WINDOW + TM] = kc_ref[0]
    kband[WINDOW + TM:] = kn_ref[0]
    vband[0:WINDOW] = vp_ref[0]
    vband[WINDOW:WINDOW + TM] = vc_ref[0]
    vband[WINDOW + TM:] = vn_ref[0]

    rows = Q_PER_KV * WINDOW
    band = 3 * WINDOW
    r = lax.broadcasted_iota(jnp.int32, (rows, band), 0) & (WINDOW - 1)
    c = lax.broadcasted_iota(jnp.int32, (rows, band), 1)
    in_window = (c >= r) & (c <= r + 2 * WINDOW)
    row_head = lax.broadcasted_iota(jnp.int32, (rows, 1), 0) // WINDOW
    for j in range(TM // WINDOW):
        blk = i * (TM // WINDOW) + j
        lo = jnp.where(blk == 0, WINDOW, 0)
        hi = jnp.where(blk == n_blocks - 1, 2 * WINDOW, band)
        valid = in_window & (c >= lo) & (c < hi)
        for kvh in range(N_KV_HEADS):
            heads = [kvh * Q_PER_KV + g for g in range(Q_PER_KV)]
            qs = jnp.concatenate(
                [q_ref[0, j * WINDOW:(j + 1) * WINDOW, h * HEAD_DIM:(h + 1) * HEAD_DIM] for h in heads], axis=0)
            kb = kband[j * WINDOW:j * WINDOW + band, kvh * HEAD_DIM:(kvh + 1) * HEAD_DIM]
            vb = vband[j * WINDOW:j * WINDOW + band, kvh * HEAD_DIM:(kvh + 1) * HEAD_DIM]
            s = lax.dot_general(qs, kb, (((1,), (1,)), ((), ())), preferred_element_type=_F32)
            s = jnp.where(valid, s, -jnp.inf)
            sink = jnp.full((rows, 1), sink_ref[heads[0]], _F32)
            for g in range(1, Q_PER_KV):
                sink = jnp.where(row_head == g, sink_ref[heads[g]], sink)
            m = jnp.maximum(jnp.max(s, axis=-1, keepdims=True), sink)
            p = jnp.exp(s - m)
            denom = jnp.sum(p, axis=-1, keepdims=True) + jnp.exp(sink - m)
            o = jnp.dot(p.astype(_BF16), vb, preferred_element_type=_F32) / denom
            for g, h in enumerate(heads):
                attn_o[j * WINDOW:(j + 1) * WINDOW, h * HEAD_DIM:(h + 1) * HEAD_DIM] = (
                    o[g * WINDOW:(g + 1) * WINDOW].astype(_BF16))

    y = jnp.dot(attn_o[...], wo_ref[...], preferred_element_type=_F32)
    gate = mod_ref[0][:, 2 * D_MODEL:]
    o_ref[0] = _layer_norm(DN_ALPHA * x_ref[0] + (1.0 + gate) * y, g_ref[...], b_ref[...])


def _attention_block(x, q, k, v, mod, w_out, sinks, ln_g, ln_b):
    batch, seq, _ = x.shape
    n_kv = N_KV_HEADS * HEAD_DIM
    per_tile = TM // WINDOW
    n_blocks = seq // WINDOW
    tok = lambda width: pl.BlockSpec((1, TM, width), lambda b, i: (b, i, 0))
    prev = pl.BlockSpec((1, WINDOW, n_kv), lambda b, i: (b, jnp.maximum(i * per_tile - 1, 0), 0))
    nxt = pl.BlockSpec((1, WINDOW, n_kv), lambda b, i: (b, jnp.minimum((i + 1) * per_tile, n_blocks - 1), 0))
    return pl.pallas_call(
        functools.partial(_attn_kernel, n_blocks=n_blocks),
        grid=(batch, seq // TM),
        in_specs=[pl.BlockSpec(memory_space=pltpu.SMEM),
                  tok(N_Q_HEADS * HEAD_DIM), prev, tok(n_kv), nxt, prev, tok(n_kv), nxt,
                  tok(D_MODEL), pl.BlockSpec((1, 1, 3 * D_MODEL), lambda b, i: (b, 0, 0)),
                  _const_spec(w_out.shape), _const_spec((1, D_MODEL)), _const_spec((1, D_MODEL))],
        out_specs=tok(D_MODEL),
        out_shape=jax.ShapeDtypeStruct(x.shape, _F32),
        scratch_shapes=[pltpu.VMEM((TM + 2 * WINDOW, n_kv), _BF16),
                        pltpu.VMEM((TM + 2 * WINDOW, n_kv), _BF16),
                        pltpu.VMEM((TM, N_Q_HEADS * HEAD_DIM), _BF16)],
        compiler_params=_params(2),
        name="attn_band_out_ln",
    )(sinks, q, k, k, k, v, v, v, x, mod, w_out, ln_g, ln_b)


def _mlp_kernel(x_ref, mod_ref, w1_ref, w2_ref, g_ref, b_ref, o_ref, acc_ref):
    x = x_ref[0]
    mod = mod_ref[0]
    shift, scale, gate = mod[:, :D_MODEL], mod[:, D_MODEL:2 * D_MODEL], mod[:, 2 * D_MODEL:]
    h = (x * (1.0 + scale) + shift).astype(_BF16)
    for c in range(D_FF // FF_CHUNK):
        cols = slice(c * FF_CHUNK, (c + 1) * FF_CHUNK)
        hid = jnp.maximum(jnp.dot(h, w1_ref[:, cols], preferred_element_type=_F32), 0.0)
        part = jnp.dot((hid * hid).astype(_BF16), w2_ref[cols, :], preferred_element_type=_F32)
        if c == 0:
            acc_ref[...] = part
        else:
            acc_ref[...] += part
    o_ref[0] = _layer_norm(DN_ALPHA * x + (1.0 + gate) * acc_ref[...], g_ref[...], b_ref[...])


def _mlp_block(x, mod, w1, w2, ln_g, ln_b):
    batch, seq, _ = x.shape
    tok = pl.BlockSpec((1, TM, D_MODEL), lambda b, i: (b, i, 0))
    return pl.pallas_call(
        _mlp_kernel,
        grid=(batch, seq // TM),
        in_specs=[tok, pl.BlockSpec((1, 1, 3 * D_MODEL), lambda b, i: (b, 0, 0)),
                  _const_spec(w1.shape), _const_spec(w2.shape),
                  _const_spec((1, D_MODEL)), _const_spec((1, D_MODEL))],
        out_specs=tok,
        out_shape=jax.ShapeDtypeStruct(x.shape, _F32),
        scratch_shapes=[pltpu.VMEM((TM, D_MODEL), _F32)],
        compiler_params=_params(2),
        name="mlp_relu2_ln",
    )(x, mod, w1, w2, ln_g, ln_b)


def _decay_rate(lam_ref):
    neg_lam = -lam_ref[...]
    softplus = jnp.maximum(neg_lam, 0.0) + jnp.log1p(jnp.exp(-jnp.abs(neg_lam)))
    return (-0.5 * LRU_C * LOG2_E) * softplus


def _lru_group(j, xc, xcb, rate, wg_ref, ba_ref, bx_ref, carry_ref, write_rows, reverse):
    cols = slice(j * GATE_GROUP, (j + 1) * GATE_GROUP)
    pre = jnp.dot(xcb[:, cols], wg_ref[j], preferred_element_type=_F32)
    t_a = jnp.tanh(pre[:, :GATE_GROUP] + ba_ref[:, cols])
    t_x = jnp.tanh(pre[:, GATE_GROUP:] + bx_ref[:, cols])
    k = rate[:, cols]
    a = jnp.exp2(k * t_a + k)
    v = 1.0 - a * a
    mult = jnp.where(v > 0.0, v * lax.rsqrt(v), 0.0)
    half_x = 0.5 * xc[:, cols]
    u = mult * (half_x * t_x + half_x)

    order = range(SEG_LEN - 1, -1, -1) if reverse else range(SEG_LEN)
    rows = lambda arr, t: arr[t * SUBLANES:(t + 1) * SUBLANES]
    h_end = p_end = None
    for t in order:
        a_t, u_t = rows(a, t), rows(u, t)
        h_end = u_t if h_end is None else a_t * h_end + u_t
        p_end = a_t if p_end is None else p_end * a_t

    sub = lax.broadcasted_iota(jnp.int32, (SUBLANES, GATE_GROUP), 0)
    for d in (1, 2, 4):
        take = (sub < SUBLANES - d) if reverse else (sub >= d)
        shift = SUBLANES - d if reverse else d
        h_nb = jnp.where(take, pltpu.roll(h_end, shift, 0), 0.0)
        p_nb = jnp.where(take, pltpu.roll(p_end, shift, 0), 1.0)
        h_end = p_end * h_nb + h_end
        p_end = p_end * p_nb

    edge = 0 if reverse else SUBLANES - 1
    entry = SUBLANES - 1 if reverse else 0
    c_in = carry_ref[edge:edge + 1, cols]
    h_true = h_end + p_end * c_in
    carry_ref[:, cols] = h_true
    h = jnp.where(sub == entry, c_in, pltpu.roll(h_true, SUBLANES - 1 if reverse else 1, 0))
    for t in order:
        h = rows(a, t) * h + rows(u, t)
        write_rows(t, cols, h)


def _rnn_fwd_kernel(x_ref, xp_ref, xn_ref, mod_ref, w_ref, cw_ref, cb_ref, wg_ref, ba_ref, bx_ref, lam_ref,
                    xc_ref, gate_ref, hf_ref, lhs, carry, *, n_tiles):
    i = pl.program_id(1)

    @pl.when(i == 0)
    def _():
        carry[...] = jnp.zeros_like(carry)

    mod = mod_ref[0]
    shift, scale = mod[:, :D_MODEL], mod[:, D_MODEL:2 * D_MODEL]
    lhs[0:TM] = (x_ref[0] * (1.0 + scale) + shift).astype(_BF16)
    halo = jnp.concatenate([xp_ref[0], xn_ref[0], jnp.zeros((SUBLANES, D_MODEL), _F32)], axis=0)
    lhs[TM:] = (halo * (1.0 + scale) + shift).astype(_BF16)

    xr_all = jnp.dot(lhs[...], w_ref[:, :D_RNN], preferred_element_type=_F32)
    gate_ref[0] = _gelu_tanh(jnp.dot(lhs[0:TM], w_ref[:, D_RNN:], preferred_element_type=_F32)).astype(_BF16)

    xr = xr_all[:TM]
    has_prev = jnp.where(i > 0, 1.0, 0.0)
    has_next = jnp.where(i < n_tiles - 1, 1.0, 0.0)
    prev2 = xr_all[TM + 7:TM + 8] * has_prev
    prev1 = xr_all[TM + 15:TM + 16] * has_prev
    next1 = xr_all[TM + 16:TM + 17] * has_next
    sub = lax.broadcasted_iota(jnp.int32, (SUBLANES, D_RNN), 0)
    row_m2 = jnp.where(sub == 0, prev2, pltpu.roll(xr[TM - 2 * SUBLANES:TM - SUBLANES], 1, 0))
    row_m1 = jnp.where(sub == 0, prev1, pltpu.roll(xr[TM - SUBLANES:], 1, 0))
    row_end = jnp.where(sub == SUBLANES - 1, next1, pltpu.roll(xr[:SUBLANES], SUBLANES - 1, 0))
    ext = jnp.concatenate([row_m2, row_m1, xr, row_end], axis=0)
    xc = cb_ref[...] + cw_ref[0:1, :] * ext[0:TM]
    for tap in range(1, CONV_W):
        xc = xc + cw_ref[tap:tap + 1, :] * ext[tap * SUBLANES:tap * SUBLANES + TM]
    xc_ref[0] = xc

    def write_rows(t, cols, h):
        hf_ref[0, t * SUBLANES:(t + 1) * SUBLANES, cols] = h

    xcb = xc.astype(_BF16)
    rate = _decay_rate(lam_ref)
    for j in range(N_GATE_GROUPS):
        _lru_group(j, xc, xcb, rate, wg_ref, ba_ref, bx_ref, carry, write_rows, reverse=False)


def _rnn_forward(xp, mod, w_in, conv_w, conv_b, wg, b_a, b_x, lam):
    batch, seq, _ = xp.shape
    n_tiles = seq // TM
    tok = lambda width: pl.BlockSpec((1, TM, width), lambda b, i: (b, i, 0))
    prev = pl.BlockSpec((1, 2 * SUBLANES, D_MODEL),
                        lambda b, i: (b, jnp.maximum(i * (TM // (2 * SUBLANES)) - 1, 0), 0))
    nxt = pl.BlockSpec((1, SUBLANES, D_MODEL),
                       lambda b, i: (b, jnp.minimum((i + 1) * SEG_LEN, seq // SUBLANES - 1), 0))
    vec = _const_spec((1, D_RNN))
    return pl.pallas_call(
        functools.partial(_rnn_fwd_kernel, n_tiles=n_tiles),
        grid=(batch, n_tiles),
        in_specs=[tok(D_MODEL), prev, nxt, pl.BlockSpec((1, 1, 3 * D_MODEL), lambda b, i: (b, 0, 0)),
                  _const_spec(w_in.shape), _const_spec((CONV_W, D_RNN)), vec,
                  _const_spec(wg.shape), vec, vec, vec],
        out_specs=[tok(D_RNN), tok(D_RNN), tok(D_RNN)],
        out_shape=[jax.ShapeDtypeStruct((batch, seq, D_RNN), _F32),
                   jax.ShapeDtypeStruct((batch, seq, D_RNN), _BF16),
                   jax.ShapeDtypeStruct((batch, seq, D_RNN), _F32)],
        scratch_shapes=[pltpu.VMEM((TM + HALO_ROWS, D_MODEL), _BF16),
                        pltpu.VMEM((SUBLANES, D_RNN), _F32)],
        compiler_params=_params(2),
        name="rnn_in_conv_fwd_scan",
    )(xp, xp, xp, mod, w_in, conv_w, conv_b, wg, b_a, b_x, lam)


def _rnn_bwd_kernel(xc_ref, gate_ref, hf_ref, x_ref, mod_ref, wg_ref, ba_ref, bx_ref, lam_ref, wo_ref,
                    g_ref, b_ref, o_ref, hb, carry):
    @pl.when(pl.program_id(1) == 0)
    def _():
        carry[...] = jnp.zeros_like(carry)

    def write_rows(t, cols, h):
        hb[t * SUBLANES:(t + 1) * SUBLANES, cols] = h

    xc = xc_ref[0]
    xcb = xc.astype(_BF16)
    rate = _decay_rate(lam_ref)
    y = None
    half = D_RNN // 2
    for part in range(2):
        for j in range(part * N_GATE_GROUPS // 2, (part + 1) * N_GATE_GROUPS // 2):
            _lru_group(j, xc, xcb, rate, wg_ref, ba_ref, bx_ref, carry, write_rows, reverse=True)
        cols = slice(part * half, (part + 1) * half)
        mixed = ((hf_ref[0, :, cols] + hb[:, cols]) * gate_ref[0, :, cols].astype(_F32)).astype(_BF16)
        prod = jnp.dot(mixed, wo_ref[cols, :], preferred_element_type=_F32)
        y = prod if y is None else y + prod
    gate = mod_ref[0][:, 2 * D_MODEL:]
    o_ref[0] = _layer_norm(DN_ALPHA * x_ref[0] + (1.0 + gate) * y, g_ref[...], b_ref[...])


def _rnn_backward(xc, gate, hf, xp, mod, wg, b_a, b_x, lam, w_out, ln_g, ln_b):
    batch, seq, _ = xp.shape
    n_tiles = seq // TM
    tok = lambda width: pl.BlockSpec((1, TM, width), lambda b, i: (b, n_tiles - 1 - i, 0))
    vec = _const_spec((1, D_RNN))
    return pl.pallas_call(
        _rnn_bwd_kernel,
        grid=(batch, n_tiles),
        in_specs=[tok(D_RNN), tok(D_RNN), tok(D_RNN), tok(D_MODEL),
                  pl.BlockSpec((1, 1, 3 * D_MODEL), lambda b, i: (b, 0, 0)),
                  _const_spec(wg.shape), vec, vec, vec, _const_spec(w_out.shape),
                  _const_spec((1, D_MODEL)), _const_spec((1, D_MODEL))],
        out_specs=tok(D_MODEL),
        out_shape=jax.ShapeDtypeStruct(xp.shape, _F32),
        scratch_shapes=[pltpu.VMEM((TM, D_RNN), _F32), pltpu.VMEM((SUBLANES, D_RNN), _F32)],
        compiler_params=_params(2),
        name="rnn_bwd_scan_out_ln",
    )(xc, gate, hf, xp, mod, wg, b_a, b_x, lam, w_out, ln_g, ln_b)


def _gate_weights(w_a, w_x):
    per_group = GATE_GROUP // RNN_BLOCK_W
    eye = jnp.eye(per_group, dtype=w_a.dtype)

    def dense(w):
        w = w.reshape(N_GATE_GROUPS, per_group, RNN_BLOCK_W, RNN_BLOCK_W)
        return jnp.einsum("jnik,nm->jnimk", w, eye).reshape(N_GATE_GROUPS, GATE_GROUP, GATE_GROUP)

    return (0.5 * jnp.concatenate([dense(w_a), dense(w_x)], axis=-1)).astype(_BF16)


def _permute_tokens(x):
    batch, seq, width = x.shape
    return x.reshape(batch, seq // TM, SUBLANES, SEG_LEN, width).swapaxes(2, 3).reshape(batch, seq, width)


def _unpermute_tokens(x):
    batch, seq, width = x.shape
    return x.reshape(batch, seq // TM, SEG_LEN, SUBLANES, width).swapaxes(2, 3).reshape(batch, seq, width)


def kernel(x, c, ada_w, ada_b, ln_g, ln_b, attn_w_in, attn_w_out, attn_sinks, rnn_w_in, rnn_conv_w, rnn_conv_b,
           rnn_w_a, rnn_b_a, rnn_w_x, rnn_b_x, rnn_lam, rnn_w_out, mlp_w1, mlp_w2):
    batch, seq, d_model = x.shape
    assert d_model == D_MODEL and seq % TM == 0
    mods = _ada_modulation(c, ada_w, ada_b)
    row = lambda v: v.reshape(1, -1)

    q, k, v = _qkv_projection(x, mods[0], attn_w_in[0].astype(_BF16))
    x = _attention_block(x, q, k, v, mods[0], attn_w_out[0].astype(_BF16), attn_sinks[0],
                         row(ln_g[0, 0]), row(ln_b[0, 0]))
    x = _mlp_block(x, mods[1], mlp_w1[0].astype(_BF16), mlp_w2[0].astype(_BF16), row(ln_g[0, 1]), row(ln_b[0, 1]))

    xp = _permute_tokens(x)
    wg_f = _gate_weights(rnn_w_a[0, 0], rnn_w_x[0, 0])
    wg_b = _gate_weights(rnn_w_a[0, 1], rnn_w_x[0, 1])
    xc, gate, hf = _rnn_forward(xp, mods[2], rnn_w_in[0].astype(_BF16), rnn_conv_w[0], row(rnn_conv_b[0]),
                                wg_f, row(0.5 * rnn_b_a[0, 0]), row(0.5 * rnn_b_x[0, 0]), row(rnn_lam[0, 0]))
    xp = _rnn_backward(xc, gate, hf, xp, mods[2], wg_b, row(0.5 * rnn_b_a[0, 1]), row(0.5 * rnn_b_x[0, 1]),
                       row(rnn_lam[0, 1]), rnn_w_out[0].astype(_BF16), row(ln_g[1, 0]), row(ln_b[1, 0]))
    xp = _mlp_block(xp, mods[3], mlp_w1[1].astype(_BF16), mlp_w2[1].astype(_BF16), row(ln_g[1, 1]), row(ln_b[1, 1]))
    return _unpermute_tokens(xp)
```
